```python
import jax, jax.numpy as jnp
from jax import lax
import numpy as np


D_MODEL = 1024
BATCH = 32
SEQ = 2048
DEPTH = 4
DEC_BATCH = 8
DEC_SEQ = 2048
PAST_LEN = 128

HEAD_DIM = 64
N_HEADS = D_MODEL // HEAD_DIM
N_KV_HEADS = N_HEADS // 4
Q_PER_KV = N_HEADS // N_KV_HEADS
D_ATTN = N_HEADS * HEAD_DIM
D_KV = N_KV_HEADS * HEAD_DIM
WINDOW = 128
ROPE_THETA = 500000.0
ROT_DIM = HEAD_DIM // 4
ATTN_SCALE = HEAD_DIM ** -0.5
D_SSM = 2 * D_MODEL
SSM_HEAD_DIM = 64
N_SSM_HEADS = D_SSM // SSM_HEAD_DIM
N_GROUPS = 4
HEADS_PER_GROUP = N_SSM_HEADS // N_GROUPS
D_STATE = 128
D_CONV = 5
CONV_PAD = D_CONV // 2
CHUNK = 128
CONV_DIM = D_SSM + 2 * N_GROUPS * D_STATE
D_MIX = D_ATTN + D_SSM
D_IN_PROJ = D_ATTN + 2 * D_KV + CONV_DIM + D_SSM + 2 * N_SSM_HEADS
D_FF = 4 * D_MODEL
EPS = 1e-6

kernel_name = 'hymba_bidir_ssd_swa_encoder'


def rms_norm(x, g):
    xf = x.astype(jnp.float32)
    out = xf * lax.rsqrt(jnp.mean(xf * xf, axis=-1, keepdims=True) + EPS) * g.astype(jnp.float32)
    return out.astype(x.dtype)


def rope_tables(seqlen):
    pos = jnp.arange(seqlen, dtype=jnp.float32)
    inv_freq = ROPE_THETA ** (-jnp.arange(0, ROT_DIM, 2, dtype=jnp.float32) / ROT_DIM)
    ang = pos[:, None] * inv_freq[None, :]
    return jnp.cos(ang)[None, :, None, :], jnp.sin(ang)[None, :, None, :]


def partial_rope(x, cos, sin):
    xr = x[..., :ROT_DIM].astype(jnp.float32)
    x1, x2 = xr[..., :ROT_DIM // 2], xr[..., ROT_DIM // 2:]
    rot = jnp.concatenate([x1 * cos - x2 * sin, x2 * cos + x1 * sin], axis=-1).astype(x.dtype)
    return jnp.concatenate([rot, x[..., ROT_DIM:]], axis=-1)


def windowed_attention(q, k, v, sink):
    bsz, seqlen = q.shape[0], q.shape[1]
    nb = seqlen // WINDOW
    pad = ((0, 0), (WINDOW, WINDOW), (0, 0), (0, 0))
    kb = jnp.pad(k, pad).reshape(bsz, nb + 2, WINDOW, N_KV_HEADS, HEAD_DIM)
    vb = jnp.pad(v, pad).reshape(bsz, nb + 2, WINDOW, N_KV_HEADS, HEAD_DIM)
    kwin = jnp.concatenate([kb[:, :-2], kb[:, 1:-1], kb[:, 2:]], axis=2)
    vwin = jnp.concatenate([vb[:, :-2], vb[:, 1:-1], vb[:, 2:]], axis=2)
    qb = q.reshape(bsz, nb, WINDOW, N_KV_HEADS, Q_PER_KV, HEAD_DIM)
    s = jnp.einsum('bnqhgd,bnkhd->bnhgqk', qb, kwin).astype(jnp.float32) * ATTN_SCALE
    qi = jnp.arange(WINDOW)[:, None]
    ki = jnp.arange(3 * WINDOW)[None, :]
    rel = ki - qi
    band = (rel >= 0) & (rel <= 2 * WINDOW)
    kpos = jnp.arange(nb)[:, None] * WINDOW + jnp.arange(3 * WINDOW)[None, :] - WINDOW
    valid = (kpos >= 0) & (kpos < seqlen)
    mask = (band[None] & valid[:, None, :])[None, :, None, None]
    s = jnp.where(mask, s, -jnp.inf)
    sk = sink.astype(jnp.float32).reshape(1, 1, N_KV_HEADS, Q_PER_KV, 1, 1)
    m = jnp.maximum(jnp.max(s, axis=-1, keepdims=True), sk)
    e = jnp.exp(s - m)
    p = e / (jnp.sum(e, axis=-1, keepdims=True) + jnp.exp(sk - m))
    o = jnp.einsum('bnhgqk,bnkhd->bnqhgd', p, vwin.astype(jnp.float32))
    return o.reshape(bsz, seqlen, D_ATTN).astype(q.dtype)


def ssd_chunked(x, dt, a, b, c):
    bsz, seqlen = x.shape[0], x.shape[1]
    nc = seqlen // CHUNK
    xg = (x * dt[..., None]).reshape(bsz, nc, CHUNK, N_GROUPS, HEADS_PER_GROUP, SSM_HEAD_DIM)
    da = (dt * a).reshape(bsz, nc, CHUNK, N_GROUPS, HEADS_PER_GROUP)
    acs = jnp.cumsum(da, axis=2)
    bc = b.reshape(bsz, nc, CHUNK, N_GROUPS, D_STATE)
    cc = c.reshape(bsz, nc, CHUNK, N_GROUPS, D_STATE)
    seg = acs[:, :, :, None] - acs[:, :, None]
    tri = jnp.tril(jnp.ones((CHUNK, CHUNK), dtype=bool))[:, :, None, None]
    decay = jnp.exp(jnp.where(tri, seg, -jnp.inf))
    cb = jnp.einsum('bcqgn,bckgn->bcqkg', cc, bc)
    y_diag = jnp.einsum('bcqkg,bcqkgh,bckghp->bcqghp', cb, decay, xg)
    decay_st = jnp.exp(acs[:, :, -1:] - acs)
    states = jnp.einsum('bckgn,bckgh,bckghp->bcghpn', bc, decay_st, xg)
    chunk_decay = jnp.exp(acs[:, :, -1])

    def step(h, inp):
        st, dec = inp
        return h * dec[..., None, None] + st, h

    h0 = jnp.zeros((bsz, N_GROUPS, HEADS_PER_GROUP, SSM_HEAD_DIM, D_STATE), dtype=states.dtype)
    _, h_prev = lax.scan(step, h0, (jnp.moveaxis(states, 1, 0), jnp.moveaxis(chunk_decay, 1, 0)))
    h_prev = jnp.moveaxis(h_prev, 0, 1)
    y_off = jnp.einsum('bcqgn,bcghpn,bcqgh->bcqghp', cc, h_prev, jnp.exp(acs))
    return (y_diag + y_off).reshape(bsz, seqlen, N_SSM_HEADS, SSM_HEAD_DIM)


def token_mixer(x, norm_mix, w_in, conv_w, conv_b, dt_bias, a_log, d_skip, ssm_norm,
                q_norm, k_norm, sink, attn_norm, w_out):
    bsz, seqlen, _ = x.shape
    h = rms_norm(x, norm_mix)
    proj = h @ w_in
    o1 = D_ATTN
    o2 = o1 + D_KV
    o3 = o2 + D_KV
    o4 = o3 + CONV_DIM
    o5 = o4 + D_SSM
    q, k, v, xbc, z, dt_raw = (proj[..., :o1], proj[..., o1:o2], proj[..., o2:o3],
                               proj[..., o3:o4], proj[..., o4:o5], proj[..., o5:])
    cos, sin = rope_tables(seqlen)
    q = partial_rope(rms_norm(q.reshape(bsz, seqlen, N_HEADS, HEAD_DIM), q_norm), cos, sin)
    k = partial_rope(rms_norm(k.reshape(bsz, seqlen, N_KV_HEADS, HEAD_DIM), k_norm), cos, sin)
    v = v.reshape(bsz, seqlen, N_KV_HEADS, HEAD_DIM)
    q = q.reshape(bsz, seqlen, N_KV_HEADS, Q_PER_KV, HEAD_DIM)
    attn = rms_norm(windowed_attention(q, k, v, sink), attn_norm)
    xbc = lax.conv_general_dilated(xbc, conv_w[:, None, :], window_strides=(1,),
                                   padding=[(CONV_PAD, CONV_PAD)],
                                   dimension_numbers=('NWC', 'WIO', 'NWC'),
                                   feature_group_count=CONV_DIM)
    xbc = jax.nn.silu(xbc + conv_b)
    xs = xbc[..., :D_SSM].reshape(bsz, seqlen, N_SSM_HEADS, SSM_HEAD_DIM)
    b = xbc[..., D_SSM:D_SSM + N_GROUPS * D_STATE].reshape(bsz, seqlen, N_GROUPS, D_STATE)
    c = xbc[..., D_SSM + N_GROUPS * D_STATE:].reshape(bsz, seqlen, N_GROUPS, D_STATE)
    dt = jax.nn.softplus(dt_raw.astype(jnp.float32).reshape(bsz, seqlen, 2, N_SSM_HEADS)
                         + dt_bias.astype(jnp.float32))
    a = -jnp.exp(a_log.astype(jnp.float32))
    y_f = ssd_chunked(xs, dt[:, :, 0], a[0], b, c)
    y_b = jnp.flip(ssd_chunked(jnp.flip(xs, 1), jnp.flip(dt[:, :, 1], 1), a[1],
                               jnp.flip(b, 1), jnp.flip(c, 1)), 1)
    y = y_f + y_b + xs * d_skip[:, None]
    y = y.reshape(bsz, seqlen, D_SSM) * jax.nn.silu(z.astype(jnp.float32))
    yg = y.reshape(bsz, seqlen, N_GROUPS, D_SSM // N_GROUPS)
    yg = yg * lax.rsqrt(jnp.mean(yg * yg, axis=-1, keepdims=True) + EPS)
    y = (yg.reshape(bsz, seqlen, D_SSM) * ssm_norm.astype(jnp.float32)).astype(x.dtype)
    return jnp.concatenate([attn, y], axis=-1) @ w_out


def run_trunk(x, norm_mix, w_in, conv_w, conv_b, dt_bias, a_log, d_skip, ssm_norm,
              q_norm, k_norm, sink, attn_norm, w_out, norm_mlp, w_up, w_down):
    for i in range(DEPTH):
        x = x + token_mixer(x, norm_mix[i], w_in[i], conv_w[i], conv_b[i], dt_bias[i], a_log[i],
                            d_skip[i], ssm_norm[i], q_norm[i], k_norm[i], sink[i], attn_norm[i],
                            w_out[i])
        h = rms_norm(x, norm_mlp[i])
        x = x + jnp.square(jax.nn.relu(h @ w_up[i])) @ w_down[i]
    return x


def setup_inputs(seed: int = 0) -> dict:
    key = jax.random.key(seed)
    ks = jax.random.split(key, 20)
    f32 = jnp.float32

    def nrm(k, shape, scale):
        return jax.random.normal(k, shape, dtype=f32) * scale

    def gain(k, shape):
        return 1.0 + 0.02 * jax.random.normal(k, shape, dtype=f32)

    dt0 = jnp.exp(jax.random.uniform(ks[6], (DEPTH, 2, N_SSM_HEADS), dtype=f32,
                                     minval=float(np.log(1e-3)), maxval=float(np.log(1e-1))))
    return {
        'x_prompt': jax.random.normal(ks[0], (BATCH, SEQ, D_MODEL), dtype=f32),
        'x_sample': jax.random.normal(ks[1], (DEC_BATCH, DEC_SEQ, D_MODEL), dtype=f32),
        'norm_mix': gain(ks[2], (DEPTH, D_MODEL)),
        'w_in': nrm(ks[3], (DEPTH, D_MODEL, D_IN_PROJ), D_MODEL ** -0.5),
        'conv_w': nrm(ks[4], (DEPTH, D_CONV, CONV_DIM), D_CONV ** -0.5),
        'conv_b': nrm(ks[5], (DEPTH, CONV_DIM), 0.02),
        'dt_bias': dt0 + jnp.log(-jnp.expm1(-dt0)),
        'a_log': jnp.log(jax.random.uniform(ks[7], (DEPTH, 2, N_SSM_HEADS), dtype=f32,
                                            minval=1.0, maxval=16.0)),
        'd_skip': gain(ks[8], (DEPTH, N_SSM_HEADS)),
        'ssm_norm': gain(ks[9], (DEPTH, D_SSM)),
        'q_norm': gain(ks[10], (DEPTH, HEAD_DIM)),
        'k_norm': gain(ks[11], (DEPTH, HEAD_DIM)),
        'sink': nrm(ks[12], (DEPTH, N_HEADS), 0.5),
        'attn_norm': gain(ks[13], (DEPTH, D_ATTN)),
        'w_out': nrm(ks[14], (DEPTH, D_MIX, D_MODEL), D_MIX ** -0.5),
        'norm_mlp': gain(ks[15], (DEPTH, D_MODEL)),
        'w_up': nrm(ks[16], (DEPTH, D_MODEL, D_FF), D_MODEL ** -0.5),
        'w_down': nrm(ks[17], (DEPTH, D_FF, D_MODEL), D_FF ** -0.5),
    }


def reference(x_prompt, x_sample, norm_mix, w_in, conv_w, conv_b, dt_bias, a_log, d_skip,
              ssm_norm, q_norm, k_norm, sink, attn_norm, w_out, norm_mlp, w_up, w_down):
    y_prompt = run_trunk(x_prompt, norm_mix, w_in, conv_w, conv_b, dt_bias, a_log, d_skip,
                         ssm_norm, q_norm, k_norm, sink, attn_norm, w_out, norm_mlp, w_up, w_down)
    y_sample = run_trunk(x_sample, norm_mix, w_in, conv_w, conv_b, dt_bias, a_log, d_skip,
                         ssm_norm, q_norm, k_norm, sink, attn_norm, w_out, norm_mlp, w_up, w_down)
    return (y_prompt, y_sample)
```

```python
import functools

import jax
import jax.numpy as jnp
import numpy as np
from jax import lax
from jax.experimental import pallas as pl
from jax.experimental.pallas import tpu as pltpu

F32 = jnp.float32
BF16 = jnp.bfloat16

D_MODEL = 1024
HEAD_DIM = 64
N_HEADS = 16
N_KV_HEADS = 4
Q_PER_KV = N_HEADS // N_KV_HEADS
D_ATTN = N_HEADS * HEAD_DIM
D_KV = N_KV_HEADS * HEAD_DIM
WINDOW = 128
ROPE_THETA = 500000.0
ROT_DIM = HEAD_DIM // 4
ROT_HALF = ROT_DIM // 2
ATTN_SCALE = HEAD_DIM ** -0.5
D_SSM = 2 * D_MODEL
SSM_HEAD_DIM = 64
N_SSM_HEADS = D_SSM // SSM_HEAD_DIM
N_GROUPS = 4
HEADS_PER_GROUP = N_SSM_HEADS // N_GROUPS
D_STATE = 128
D_CONV = 5
CONV_PAD = D_CONV // 2
CHUNK = 128
CONV_DIM = D_SSM + 2 * N_GROUPS * D_STATE
D_FF = 4 * D_MODEL
EPS = 1e-6

LANES = 128
BF16_ROWS = 16
VMEM_LIMIT_BYTES = 56 * 1024 * 1024

Q_G = Q_PER_KV * HEAD_DIM
KV_G = 2 * HEAD_DIM
XS_G = D_SSM // N_GROUPS
XBC_G = XS_G + 2 * D_STATE
PAIRS = XS_G // LANES
DT_G = 2 * HEADS_PER_GROUP

OFF_Q = 0
OFF_KV = OFF_Q + D_ATTN
OFF_XBC = OFF_KV + 2 * D_KV
OFF_Z = OFF_XBC + CONV_DIM
OFF_DT = OFF_Z + D_SSM
D_IN = OFF_DT + N_GROUPS * LANES

TOKEN_TILE = 512


def _resident_spec(shape):
    return pl.BlockSpec(shape, lambda *_: (0,) * len(shape), pipeline_mode=pl.Buffered(1))


def _dot(a, b):
    return jnp.dot(a, b, preferred_element_type=F32)


def _dot_nt(a, b):
    return lax.dot_general(a, b, (((1,), (1,)), ((), ())), preferred_element_type=F32)


def _sigmoid(x):
    return 1.0 / (1.0 + jnp.exp(-x))


def _split3(x):
    p1 = x.astype(BF16)
    r1 = x - p1.astype(F32)
    p2 = r1.astype(BF16)
    p3 = (r1 - p2.astype(F32)).astype(BF16)
    return p1, p2, p3


def _rope(v, tab_ref):
    return (v * tab_ref[0]
            + pltpu.roll(v, ROT_HALF, 1) * tab_ref[1]
            + pltpu.roll(v, LANES - ROT_HALF, 1) * tab_ref[2])


def _in_proj_kernel(x_ref, g_ref, w_ref, gq_ref, gkv_ref, rq_ref, rkv_ref, ones_ref,
                    q_ref, kv_ref, xbc_ref, z_ref, dt_ref):
    x = x_ref[...]
    h = (x * lax.rsqrt(jnp.mean(x * x, axis=-1, keepdims=True) + EPS) * g_ref[...]).astype(BF16)

    def head_rsqrt(a):
        ss = _dot((a * a).astype(BF16), ones_ref[...])
        return lax.rsqrt(ss * (1.0 / HEAD_DIM) + EPS)

    step = 2 * LANES
    for c0 in range(0, D_ATTN, step):
        a = _dot(h, w_ref[:, OFF_Q + c0:OFF_Q + c0 + step])
        an = a * head_rsqrt(a) * gq_ref[:, c0:c0 + step]
        for t in range(0, step, LANES):
            q_ref[:, c0 + t:c0 + t + LANES] = _rope(an[:, t:t + LANES], rq_ref).astype(BF16)

    is_k = lax.broadcasted_iota(jnp.int32, (1, step), 1) % LANES < HEAD_DIM
    for c0 in range(0, 2 * D_KV, step):
        a = _dot(h, w_ref[:, OFF_KV + c0:OFF_KV + c0 + step])
        an = a * jnp.where(is_k, head_rsqrt(a), 1.0) * gkv_ref[:, c0:c0 + step]
        for t in range(0, step, LANES):
            kv_ref[:, c0 + t:c0 + t + LANES] = _rope(an[:, t:t + LANES], rkv_ref).astype(BF16)

    step = 4 * LANES
    for c0 in range(0, CONV_DIM, step):
        xbc_ref[:, c0:c0 + step] = _dot(h, w_ref[:, OFF_XBC + c0:OFF_XBC + c0 + step]).astype(BF16)
    for c0 in range(0, D_SSM, step):
        z_ref[:, c0:c0 + step] = _dot(h, w_ref[:, OFF_Z + c0:OFF_Z + c0 + step]).astype(BF16)
    dt_ref[...] = _dot(h, w_ref[:, OFF_DT:D_IN])


def _in_proj(x, g, w, gq, gkv, rope_q, rope_kv, ones, seqlen):
    tokens = x.shape[0]
    tm = min(TOKEN_TILE, seqlen)
    tiles_per_seq = seqlen // tm
    row = lambda i: (i, 0)
    const = lambda i: (0, 0)
    rope_map = lambda i: (0, i % tiles_per_seq, 0)
    return pl.pallas_call(
        _in_proj_kernel,
        grid=(tokens // tm,),
        in_specs=[
            pl.BlockSpec((tm, D_MODEL), row),
            pl.BlockSpec((1, D_MODEL), const),
            _resident_spec((D_MODEL, D_IN)),
            pl.BlockSpec((1, D_ATTN), const),
            pl.BlockSpec((1, 2 * D_KV), const),
            pl.BlockSpec((3, tm, LANES), rope_map),
            pl.BlockSpec((3, tm, LANES), rope_map),
            pl.BlockSpec((2 * LANES, 2 * LANES), const),
        ],
        out_specs=[
            pl.BlockSpec((tm, D_ATTN), row),
            pl.BlockSpec((tm, 2 * D_KV), row),
            pl.BlockSpec((tm, CONV_DIM), row),
            pl.BlockSpec((tm, D_SSM), row),
            pl.BlockSpec((tm, N_GROUPS * LANES), row),
        ],
        out_shape=[
            jax.ShapeDtypeStruct((tokens, D_ATTN), BF16),
            jax.ShapeDtypeStruct((tokens, 2 * D_KV), BF16),
            jax.ShapeDtypeStruct((tokens, CONV_DIM), BF16),
            jax.ShapeDtypeStruct((tokens, D_SSM), BF16),
            jax.ShapeDtypeStruct((tokens, N_GROUPS * LANES), F32),
        ],
        compiler_params=pltpu.CompilerParams(
            dimension_semantics=("arbitrary",), vmem_limit_bytes=VMEM_LIMIT_BYTES),
        name="in_proj",
    )(x, g, w, gq, gkv, rope_q, rope_kv, ones)


def _pair_expand(col, h0, lane_lo):
    return jnp.where(lane_lo, col[:, h0:h0 + 1], col[:, h0 + 1:h0 + 2])


def _mixer_kernel(q_ref, kv_ref, xbc_ref, z_ref, dt_ref,
                  cw_ref, cb_ref, dtb_ref, alog_ref, dskip_ref, snorm_ref, sink_ref,
                  attn_ref, y_ref,
                  xs_s, bt_s, c_s, dt_s, acs_s, dtr_s, acsr_s, hb_s, hrun_s,
                  ka_s, kb_s, va_s, vb_s, *, seqlen):
    n_chunks = seqlen // CHUNK
    hpg = HEADS_PER_GROUP
    lane = lax.broadcasted_iota(jnp.int32, (CHUNK, LANES), 1)
    row = lax.broadcasted_iota(jnp.int32, (CHUNK, LANES), 0)
    lane_lo = lane < HEAD_DIM
    tril_incl = lane <= row
    tri_bf16 = jnp.where(tril_incl, 1.0, 0.0).astype(BF16)

    zpad = jnp.zeros((WINDOW, LANES), BF16)
    for s in (ka_s, kb_s, va_s, vb_s):
        s[0:WINDOW, :] = zpad
        s[seqlen + WINDOW:seqlen + 2 * WINDOW, :] = zpad

    def fill_kv(c, carry):
        r0 = pl.multiple_of(c * CHUNK, CHUNK)
        kv = kv_ref[pl.ds(r0, CHUNK), :]
        swapped = jnp.concatenate([kv[:, HEAD_DIM:], kv[:, :HEAD_DIM]], axis=1)
        zero = jnp.zeros_like(kv)
        dst = pl.ds(r0 + WINDOW, CHUNK)
        ka_s[dst, :] = jnp.where(lane_lo, kv, zero)
        kb_s[dst, :] = jnp.where(lane_lo, zero, swapped)
        va_s[dst, :] = jnp.where(lane_lo, swapped, zero)
        vb_s[dst, :] = jnp.where(lane_lo, zero, kv)
        return carry

    lax.fori_loop(0, n_chunks, fill_kv, 0)

    kidx = lax.broadcasted_iota(jnp.int32, (WINDOW, 3 * WINDOW), 1)
    qidx = lax.broadcasted_iota(jnp.int32, (WINDOW, 3 * WINDOW), 0)
    rel = kidx - qidx
    band = (rel >= 0) & (rel <= 2 * WINDOW)

    def attn_block(n, carry):
        r0 = pl.multiple_of(n * WINDOW, WINDOW)
        kpos = kidx + (n - 1) * WINDOW
        mask = band & (kpos >= 0) & (kpos < seqlen)
        win = pl.ds(r0, 3 * WINDOW)
        for t in range(Q_G // LANES):
            qp = q_ref[pl.ds(r0, WINDOW), t * LANES:(t + 1) * LANES]
            out = None
            for half, (k_s, v_s) in enumerate(((ka_s, va_s), (kb_s, vb_s))):
                hd = 2 * t + half
                s = _dot_nt(qp, k_s[win, :])
                s = jnp.where(mask, s, -jnp.inf)
                sk = sink_ref[hd:hd + 1, 0:1]
                m = jnp.maximum(jnp.max(s, axis=-1, keepdims=True), sk)
                e = jnp.exp(s - m)
                denom = jnp.sum(e, axis=-1, keepdims=True) + jnp.exp(sk - m)
                o = _dot(e.astype(BF16), v_s[win, :]) * (1.0 / denom)
                out = o if out is None else out + o
            attn_ref[pl.ds(r0, WINDOW), t * LANES:(t + 1) * LANES] = out.astype(BF16)
        return carry

    lax.fori_loop(0, n_chunks, attn_block, 0)

    a_row = -jnp.exp(alog_ref[...])
    is_fwd = lane < hpg

    def prep(c, carry):
        r0 = pl.multiple_of(c * CHUNK, CHUNK)
        main = xbc_ref[pl.ds(r0, CHUNK), :].astype(F32)
        p0 = pl.multiple_of(jnp.maximum(r0 - BF16_ROWS, 0), BF16_ROWS)
        n0 = pl.multiple_of(jnp.minimum(r0 + CHUNK, seqlen - BF16_ROWS), BF16_ROWS)
        prev = xbc_ref[pl.ds(p0, BF16_ROWS), :].astype(F32)
        nxt = xbc_ref[pl.ds(n0, BF16_ROWS), :].astype(F32)
        prev = jnp.where(c > 0, prev, 0.0)
        nxt = jnp.where(c < n_chunks - 1, nxt, 0.0)
        ext = jnp.concatenate([prev, main, nxt], axis=0)
        acc = jnp.broadcast_to(cb_ref[...], (CHUNK, XBC_G))
        for j in range(D_CONV):
            lo = BF16_ROWS - CONV_PAD + j
            acc = acc + ext[lo:lo + CHUNK, :] * cw_ref[j:j + 1, :]
        act = acc * _sigmoid(acc)
        rows = pl.ds(r0, CHUNK)
        xs_s[rows, :] = act[:, :XS_G]
        bt_s[c] = act[:, XS_G:XS_G + D_STATE].T.astype(BF16)
        c_s[rows, :] = act[:, XS_G + D_STATE:].astype(BF16)

        raw = dt_ref[rows, :] + dtb_ref[...]
        dt = jnp.maximum(raw, 0.0) + jnp.log1p(jnp.exp(-jnp.abs(raw)))
        da = dt * a_row
        d1, d2, d3 = _split3(da)
        incl = _dot(tri_bf16, d1) + _dot(tri_bf16, d2) + _dot(tri_bf16, d3)
        total = incl[CHUNK - 1:CHUNK, :]
        acs = jnp.where(is_fwd, incl, total - incl + da)
        dt_s[rows, :] = dt
        acs_s[rows, :] = acs
        dtr_s[c] = dt.T[0:DT_G, :]
        acsr_s[c] = acs.T[0:DT_G, :]
        return carry

    lax.fori_loop(0, n_chunks, prep, 0)

    hrun_s[...] = jnp.zeros_like(hrun_s)

    def chunk_states(c, first_head, anchor_row):
        rows = pl.ds(pl.multiple_of(c * CHUNK, CHUNK), CHUNK)
        acs = acs_s[rows, :]
        edge = acs[anchor_row:anchor_row + 1, :]
        wst = dt_s[rows, :] * jnp.exp(edge - acs)
        dec = jnp.exp(edge)
        xw, decs = [], []
        for j in range(PAIRS):
            h0 = first_head + 2 * j
            xw.append((xs_s[rows, j * LANES:(j + 1) * LANES] * _pair_expand(wst, h0, lane_lo))
                      .astype(BF16))
            decs.append(_pair_expand(dec, h0, lane_lo[0:1, :]))
        st = _dot(bt_s[c], jnp.concatenate(xw, axis=1))
        return st, jnp.concatenate(decs, axis=1)

    def bwd(i, carry):
        c = n_chunks - 1 - i
        hb_s[c] = hrun_s[...].astype(BF16)
        st, dec = chunk_states(c, hpg, 0)
        hrun_s[...] = hrun_s[...] * dec + st
        return carry

    lax.fori_loop(0, n_chunks, bwd, 0)

    hrun_s[...] = jnp.zeros_like(hrun_s)
    diag = lane == row

    def fwd(c, carry):
        rows = pl.ds(pl.multiple_of(c * CHUNK, CHUNK), CHUNK)
        cmat = c_s[rows, :]
        cb = _dot(cmat, bt_s[c])
        acs = acs_s[rows, :]
        acs_r = acsr_s[c]
        dt_r = dtr_s[c]
        eacs = jnp.exp(acs)
        hcat = jnp.concatenate([hrun_s[...].astype(BF16), hb_s[c]], axis=1)
        yoff = _dot(cmat, hcat)
        pieces = []
        ssq = jnp.zeros((CHUNK, 1), F32)
        for j in range(PAIRS):
            ws = []
            for h in (2 * j, 2 * j + 1):
                seg = jnp.where(tril_incl,
                                acs[:, h:h + 1] - acs_r[h:h + 1, :],
                                acs[:, hpg + h:hpg + h + 1] - acs_r[hpg + h:hpg + h + 1, :])
                dtf = dt_r[h:h + 1, :]
                dtb = dt_r[hpg + h:hpg + h + 1, :]
                dsel = jnp.where(diag, dtf + dtb, jnp.where(tril_incl, dtf, dtb))
                ws.append((cb * jnp.exp(seg) * dsel).astype(BF16))
            xs = xs_s[rows, j * LANES:(j + 1) * LANES]
            xsb = xs.astype(BF16)
            zero = jnp.zeros_like(xsb)
            rhs = jnp.concatenate([jnp.where(lane_lo, xsb, zero), jnp.where(lane_lo, zero, xsb)],
                                  axis=0)
            y = _dot(jnp.concatenate(ws, axis=1), rhs)
            y = y + yoff[:, j * LANES:(j + 1) * LANES] * _pair_expand(eacs, 2 * j, lane_lo)
            y = y + (yoff[:, XS_G + j * LANES:XS_G + (j + 1) * LANES]
                     * _pair_expand(eacs, hpg + 2 * j, lane_lo))
            y = y + xs * dskip_ref[:, j * LANES:(j + 1) * LANES]
            z = z_ref[rows, j * LANES:(j + 1) * LANES].astype(F32)
            y = y * (z * _sigmoid(z))
            ssq = ssq + jnp.sum(y * y, axis=-1, keepdims=True)
            pieces.append(y)
        scale = lax.rsqrt(ssq * (1.0 / XS_G) + EPS)
        for j in range(PAIRS):
            cols = slice(j * LANES, (j + 1) * LANES)
            y_ref[rows, cols] = (pieces[j] * scale * snorm_ref[:, cols]).astype(BF16)
        st, dec = chunk_states(c, 0, CHUNK - 1)
        hrun_s[...] = hrun_s[...] * dec + st
        return carry

    lax.fori_loop(0, n_chunks, fwd, 0)


def _mixer(q, kv, xbc, z, dt, cw, cb, dtb, alog, dskip, snorm, sink, batch, seqlen):
    n_chunks = seqlen // CHUNK
    seq_col = lambda b, g: (b, 0, g)
    grp = lambda b, g: (g, 0, 0)
    sq = pl.Squeezed()
    return pl.pallas_call(
        functools.partial(_mixer_kernel, seqlen=seqlen),
        grid=(batch, N_GROUPS),
        in_specs=[
            pl.BlockSpec((sq, seqlen, Q_G), seq_col),
            pl.BlockSpec((sq, seqlen, KV_G), seq_col),
            pl.BlockSpec((sq, seqlen, XBC_G), seq_col),
            pl.BlockSpec((sq, seqlen, XS_G), seq_col),
            pl.BlockSpec((sq, seqlen, LANES), seq_col),
            pl.BlockSpec((sq, D_CONV, XBC_G), grp),
            pl.BlockSpec((sq, 1, XBC_G), grp),
            pl.BlockSpec((sq, 1, LANES), grp),
            pl.BlockSpec((sq, 1, LANES), grp),
            pl.BlockSpec((sq, 1, XS_G), grp),
            pl.BlockSpec((sq, 1, XS_G), grp),
            pl.BlockSpec((sq, Q_PER_KV, LANES), grp),
        ],
        out_specs=[
            pl.BlockSpec((sq, seqlen, Q_G), seq_col),
            pl.BlockSpec((sq, seqlen, XS_G), seq_col),
        ],
        out_shape=[
            jax.ShapeDtypeStruct((batch, seqlen, D_ATTN), BF16),
            jax.ShapeDtypeStruct((batch, seqlen, D_SSM), BF16),
        ],
        scratch_shapes=[
            pltpu.VMEM((seqlen, XS_G), F32),
            pltpu.VMEM((n_chunks, D_STATE, CHUNK), BF16),
            pltpu.VMEM((seqlen, D_STATE), BF16),
            pltpu.VMEM((seqlen, LANES), F32),
            pltpu.VMEM((seqlen, LANES), F32),
            pltpu.VMEM((n_chunks, DT_G, CHUNK), F32),
            pltpu.VMEM((n_chunks, DT_G, CHUNK), F32),
            pltpu.VMEM((n_chunks, D_STATE, XS_G), BF16),
            pltpu.VMEM((D_STATE, XS_G), F32),
            pltpu.VMEM((seqlen + 2 * WINDOW, LANES), BF16),
            pltpu.VMEM((seqlen + 2 * WINDOW, LANES), BF16),
            pltpu.VMEM((seqlen + 2 * WINDOW, LANES), BF16),
            pltpu.VMEM((seqlen + 2 * WINDOW, LANES), BF16),
        ],
        compiler_params=pltpu.CompilerParams(
            dimension_semantics=("arbitrary", "arbitrary"), vmem_limit_bytes=VMEM_LIMIT_BYTES),
        name="mixer",
    )(q, kv, xbc, z, dt, cw, cb, dtb, alog, dskip, snorm, sink)


def _out_ffn_kernel(x_ref, attn_ref, y_ref, ga_ref, wo_ref, gm_ref, wu_ref, wd_ref, o_ref):
    a = attn_ref[...].astype(F32)
    an = (a * lax.rsqrt(jnp.mean(a * a, axis=-1, keepdims=True) + EPS) * ga_ref[...]).astype(BF16)
    x1 = x_ref[...] + _dot(an, wo_ref[0:D_ATTN, :]) + _dot(y_ref[...], wo_ref[D_ATTN:, :])
    h = (x1 * lax.rsqrt(jnp.mean(x1 * x1, axis=-1, keepdims=True) + EPS) * gm_ref[...]).astype(BF16)
    o_ref[...] = x1
    step = D_FF // 4
    for c0 in range(0, D_FF, step):
        u = jnp.maximum(_dot(h, wu_ref[:, c0:c0 + step]), 0.0)
        o_ref[...] += _dot((u * u).astype(BF16), wd_ref[c0:c0 + step, :])


def _out_ffn(x, attn, y, ga, wo, gm, wu, wd, seqlen):
    tokens = x.shape[0]
    tm = min(TOKEN_TILE, seqlen)
    row = lambda i: (i, 0)
    const = lambda i: (0, 0)
    return pl.pallas_call(
        _out_ffn_kernel,
        grid=(tokens // tm,),
        in_specs=[
            pl.BlockSpec((tm, D_MODEL), row),
            pl.BlockSpec((tm, D_ATTN), row),
            pl.BlockSpec((tm, D_SSM), row),
            pl.BlockSpec((1, D_ATTN), const),
            _resident_spec((D_ATTN + D_SSM, D_MODEL)),
            pl.BlockSpec((1, D_MODEL), const),
            _resident_spec((D_MODEL, D_FF)),
            _resident_spec((D_FF, D_MODEL)),
        ],
        out_specs=pl.BlockSpec((tm, D_MODEL), row),
        out_shape=jax.ShapeDtypeStruct((tokens, D_MODEL), F32),
        compiler_params=pltpu.CompilerParams(
            dimension_semantics=("arbitrary",), vmem_limit_bytes=VMEM_LIMIT_BYTES),
        name="out_ffn",
    )(x, attn, y, ga, wo, gm, wu, wd)


def _rope_tables(seqlen):
    pos = jnp.arange(seqlen, dtype=F32)
    inv_freq = ROPE_THETA ** (-jnp.arange(0, ROT_DIM, 2, dtype=F32) / ROT_DIM)
    ang = pos[:, None] * inv_freq[None, :]
    cos, sin = jnp.cos(ang), jnp.sin(ang)
    ones = jnp.ones((seqlen, HEAD_DIM - ROT_DIM), F32)
    zeros = jnp.zeros((seqlen, HEAD_DIM - ROT_DIM), F32)
    zhalf = jnp.zeros((seqlen, ROT_HALF), F32)
    head = jnp.stack([
        jnp.concatenate([cos, cos, ones], axis=1),
        jnp.concatenate([zhalf, sin, zeros], axis=1),
        jnp.concatenate([-sin, zhalf, zeros], axis=1),
    ])
    ident = jnp.stack([jnp.ones((seqlen, HEAD_DIM), F32),
                       jnp.zeros((seqlen, HEAD_DIM), F32),
                       jnp.zeros((seqlen, HEAD_DIM), F32)])
    rope_q = jnp.concatenate([head, head], axis=2) * ATTN_SCALE
    rope_kv = jnp.concatenate([head, ident], axis=2)
    return rope_q, rope_kv


def _head_sum_matrix():
    blk = np.arange(2 * LANES) // HEAD_DIM
    return jnp.asarray(blk[:, None] == blk[None, :], dtype=BF16)


def _regroup(w_in, conv_w, conv_b, dt_bias, a_log, d_skip, ssm_norm, q_norm, k_norm, sink):
    depth = w_in.shape[0]
    o1 = D_ATTN
    o2 = o1 + D_KV
    o3 = o2 + D_KV
    o4 = o3 + CONV_DIM
    o5 = o4 + D_SSM

    def group_xbc(t):
        lead = t.shape[:-1]
        xs = t[..., :D_SSM].reshape(*lead, N_GROUPS, XS_G)
        b = t[..., D_SSM:D_SSM + N_GROUPS * D_STATE].reshape(*lead, N_GROUPS, D_STATE)
        c = t[..., D_SSM + N_GROUPS * D_STATE:].reshape(*lead, N_GROUPS, D_STATE)
        return jnp.concatenate([xs, b, c], axis=-1)

    def group_dt(t):
        lead = t.shape[:-1]
        t = t.reshape(*lead, 2, N_GROUPS, HEADS_PER_GROUP)
        t = jnp.moveaxis(t, -3, -2).reshape(*lead, N_GROUPS, DT_G)
        pad = [(0, 0)] * (t.ndim - 1) + [(0, LANES - DT_G)]
        return jnp.pad(t, pad)

    wq = w_in[..., :o1]
    wk = w_in[..., o1:o2].reshape(depth, D_MODEL, N_KV_HEADS, HEAD_DIM)
    wv = w_in[..., o2:o3].reshape(depth, D_MODEL, N_KV_HEADS, HEAD_DIM)
    wkv = jnp.concatenate([wk, wv], axis=-1).reshape(depth, D_MODEL, 2 * D_KV)
    wxbc = group_xbc(w_in[..., o3:o4]).reshape(depth, D_MODEL, CONV_DIM)
    wz = w_in[..., o4:o5]
    wdt = group_dt(w_in[..., o5:]).reshape(depth, D_MODEL, N_GROUPS * LANES)
    w_all = jnp.concatenate([wq, wkv, wxbc, wz, wdt], axis=-1).astype(BF16)

    cw = jnp.moveaxis(group_xbc(conv_w), -2, 1)
    cb = group_xbc(conv_b)[:, :, None, :]
    dtb = group_dt(dt_bias.reshape(depth, 2 * N_SSM_HEADS))[:, :, None, :]
    alog = group_dt(a_log.reshape(depth, 2 * N_SSM_HEADS))[:, :, None, :]
    dskip = jnp.repeat(d_skip, SSM_HEAD_DIM, axis=-1).reshape(depth, N_GROUPS, 1, XS_G)
    snorm = ssm_norm.reshape(depth, N_GROUPS, 1, XS_G)
    gq = jnp.tile(q_norm, (1, N_HEADS))[:, None, :]
    gkv = jnp.tile(jnp.concatenate([k_norm, jnp.ones_like(k_norm)], axis=-1),
                   (1, N_KV_HEADS))[:, None, :]
    sink_b = jnp.broadcast_to(sink.reshape(depth, N_KV_HEADS, Q_PER_KV, 1),
                              (depth, N_KV_HEADS, Q_PER_KV, LANES))
    return w_all, cw, cb, dtb, alog, dskip, snorm, gq, gkv, sink_b


def _trunk(x, params, tables):
    batch, seqlen, _ = x.shape
    (norm_mix, w_all, cw, cb, dtb, alog, dskip, snorm, gq, gkv, sink_b, attn_norm, w_out,
     norm_mlp, w_up, w_down) = params
    rope_q, rope_kv, ones = tables
    tokens = batch * seqlen
    x = x.reshape(tokens, D_MODEL)
    for i in range(w_all.shape[0]):
        q, kv, xbc, z, dt = _in_proj(x, norm_mix[i][None, :], w_all[i], gq[i], gkv[i],
                                     rope_q, rope_kv, ones, seqlen)
        shape3 = lambda t: t.reshape(batch, seqlen, t.shape[-1])
        attn, y = _mixer(shape3(q), shape3(kv), shape3(xbc), shape3(z), shape3(dt),
                         cw[i], cb[i], dtb[i], alog[i], dskip[i], snorm[i], sink_b[i],
                         batch, seqlen)
        x = _out_ffn(x, attn.reshape(tokens, D_ATTN), y.reshape(tokens, D_SSM),
                     attn_norm[i][None, :], w_out[i], norm_mlp[i][None, :], w_up[i], w_down[i],
                     seqlen)
    return x.reshape(batch, seqlen, D_MODEL)


def kernel(x_prompt, x_sample, norm_mix, w_in, conv_w, conv_b, dt_bias, a_log, d_skip, ssm_norm,
           q_norm, k_norm, sink, attn_norm, w_out, norm_mlp, w_up, w_down):
    (w_all, cw, cb, dtb, alog, dskip, snorm, gq, gkv, sink_b) = _regroup(
        w_in, conv_w, conv_b, dt_bias, a_log, d_skip, ssm_norm, q_norm, k_norm, sink)
    params = (norm_mix, w_all, cw, cb, dtb, alog, dskip, snorm, gq, gkv, sink_b, attn_norm,
              w_out.astype(BF16), norm_mlp, w_up.astype(BF16), w_down.astype(BF16))
    ones = _head_sum_matrix()
    outs = []
    for x in (x_prompt, x_sample):
        rope_q, rope_kv = _rope_tables(x.shape[1])
        outs.append(_trunk(x, params, (rope_q, rope_kv, ones)))
    return tuple(outs)
```

```python
import functools
import math

import jax
import jax.numpy as jnp
import numpy as np
from jax import lax
from jax.experimental import pallas as pl
from jax.experimental.pallas import tpu as pltpu

F32 = jnp.float32
BF16 = jnp.bfloat16

D_MODEL = 1024
HEAD_DIM = 64
N_HEADS = 16
N_KV_HEADS = 4
Q_PER_KV = N_HEADS // N_KV_HEADS
D_ATTN = N_HEADS * HEAD_DIM
D_KV = N_KV_HEADS * HEAD_DIM
WINDOW = 128
ROPE_THETA = 500000.0
ROT_DIM = HEAD_DIM // 4
ROT_HALF = ROT_DIM // 2
ATTN_SCALE = HEAD_DIM ** -0.5
D_SSM = 2 * D_MODEL
SSM_HEAD_DIM = 64
N_SSM_HEADS = D_SSM // SSM_HEAD_DIM
N_GROUPS = 4
HEADS_PER_GROUP = N_SSM_HEADS // N_GROUPS
D_STATE = 128
D_CONV = 5
CONV_PAD = D_CONV // 2
CHUNK = 128
CONV_DIM = D_SSM + 2 * N_GROUPS * D_STATE
D_FF = 4 * D_MODEL
EPS = 1e-6
LOG2E = math.log2(math.e)

LANES = 128
BF16_ROWS = 16
VMEM_LIMIT_BYTES = 56 * 1024 * 1024

Q_G = Q_PER_KV * HEAD_DIM
KV_G = 2 * HEAD_DIM
XS_G = D_SSM // N_GROUPS
XBC_G = XS_G + 2 * D_STATE
PAIRS = XS_G // LANES
DT_G = 2 * HEADS_PER_GROUP

OFF_Q = 0
OFF_KV = OFF_Q + D_ATTN
OFF_XBC = OFF_KV + 2 * D_KV
OFF_Z = OFF_XBC + CONV_DIM
OFF_DT = OFF_Z + D_SSM
D_IN = OFF_DT + N_GROUPS * LANES

TOKEN_TILE = 512


def _resident_spec(shape):
    return pl.BlockSpec(shape, lambda *_: (0,) * len(shape), pipeline_mode=pl.Buffered(1))


def _dot(a, b):
    return jnp.dot(a, b, preferred_element_type=F32)


def _dot_nt(a, b):
    return lax.dot_general(a, b, (((1,), (1,)), ((), ())), preferred_element_type=F32)


def _sigmoid(x):
    return 1.0 / (1.0 + jnp.exp(-x))


def _split3(x):
    p1 = x.astype(BF16)
    r1 = x - p1.astype(F32)
    p2 = r1.astype(BF16)
    p3 = (r1 - p2.astype(F32)).astype(BF16)
    return p1, p2, p3


def _rope(v, tab_ref):
    return (v * tab_ref[0]
            + pltpu.roll(v, ROT_HALF, 1) * tab_ref[1]
            + pltpu.roll(v, LANES - ROT_HALF, 1) * tab_ref[2])


def _in_proj_kernel(x_ref, g_ref, w_ref, gq_ref, gkv_ref, rq_ref, rkv_ref, ones_ref,
                    q_ref, kv_ref, xbc_ref, z_ref, dt_ref):
    x = x_ref[...]
    h = (x * lax.rsqrt(jnp.mean(x * x, axis=-1, keepdims=True) + EPS) * g_ref[...]).astype(BF16)

    def head_rsqrt(a):
        ss = _dot((a * a).astype(BF16), ones_ref[...])
        return lax.rsqrt(ss * (1.0 / HEAD_DIM) + EPS)

    step = 2 * LANES
    for c0 in range(0, D_ATTN, step):
        a = _dot(h, w_ref[:, OFF_Q + c0:OFF_Q + c0 + step])
        an = a * head_rsqrt(a) * gq_ref[:, c0:c0 + step]
        for t in range(0, step, LANES):
            q_ref[:, c0 + t:c0 + t + LANES] = _rope(an[:, t:t + LANES], rq_ref).astype(BF16)

    is_k = lax.broadcasted_iota(jnp.int32, (1, step), 1) % LANES < HEAD_DIM
    for c0 in range(0, 2 * D_KV, step):
        a = _dot(h, w_ref[:, OFF_KV + c0:OFF_KV + c0 + step])
        an = a * jnp.where(is_k, head_rsqrt(a), 1.0) * gkv_ref[:, c0:c0 + step]
        for t in range(0, step, LANES):
            kv_ref[:, c0 + t:c0 + t + LANES] = _rope(an[:, t:t + LANES], rkv_ref).astype(BF16)

    step = 4 * LANES
    for c0 in range(0, CONV_DIM, step):
        xbc_ref[:, c0:c0 + step] = _dot(h, w_ref[:, OFF_XBC + c0:OFF_XBC + c0 + step]).astype(BF16)
    for c0 in range(0, D_SSM, step):
        z_ref[:, c0:c0 + step] = _dot(h, w_ref[:, OFF_Z + c0:OFF_Z + c0 + step]).astype(BF16)
    dt_ref[...] = _dot(h, w_ref[:, OFF_DT:D_IN])


def _in_proj(x, g, w, gq, gkv, rope_q, rope_kv, ones, seqlen):
    tokens = x.shape[0]
    tm = min(TOKEN_TILE, seqlen)
    tiles_per_seq = seqlen // tm
    row = lambda i: (i, 0)
    const = lambda i: (0, 0)
    rope_map = lambda i: (0, i % tiles_per_seq, 0)
    return pl.pallas_call(
        _in_proj_kernel,
        grid=(tokens // tm,),
        in_specs=[
            pl.BlockSpec((tm, D_MODEL), row),
            pl.BlockSpec((1, D_MODEL), const),
            _resident_spec((D_MODEL, D_IN)),
            pl.BlockSpec((1, D_ATTN), const),
            pl.BlockSpec((1, 2 * D_KV), const),
            pl.BlockSpec((3, tm, LANES), rope_map),
            pl.BlockSpec((3, tm, LANES), rope_map),
            pl.BlockSpec((2 * LANES, 2 * LANES), const),
        ],
        out_specs=[
            pl.BlockSpec((tm, D_ATTN), row),
            pl.BlockSpec((tm, 2 * D_KV), row),
            pl.BlockSpec((tm, CONV_DIM), row),
            pl.BlockSpec((tm, D_SSM), row),
            pl.BlockSpec((tm, N_GROUPS * LANES), row),
        ],
        out_shape=[
            jax.ShapeDtypeStruct((tokens, D_ATTN), BF16),
            jax.ShapeDtypeStruct((tokens, 2 * D_KV), BF16),
            jax.ShapeDtypeStruct((tokens, CONV_DIM), BF16),
            jax.ShapeDtypeStruct((tokens, D_SSM), BF16),
            jax.ShapeDtypeStruct((tokens, N_GROUPS * LANES), F32),
        ],
        compiler_params=pltpu.CompilerParams(
            dimension_semantics=("arbitrary",), vmem_limit_bytes=VMEM_LIMIT_BYTES),
        name="in_proj",
    )(x, g, w, gq, gkv, rope_q, rope_kv, ones)


SPLIT = 16
SEG_ONES = 3 * SPLIT
SEG_COLS = 2 * HEADS_PER_GROUP * CHUNK


def _pair_expand(col, h0, lane_lo):
    return jnp.where(lane_lo, col[:, h0:h0 + 1], col[:, h0 + 1:h0 + 2])


def _mixer_kernel(q_ref, kv_ref, xbc_ref, z_ref, dt_ref,
                  cw_ref, cb_ref, dtb_ref, alog_ref, dskip_ref, snorm_ref, sink_ref,
                  attn_ref, y_ref,
                  xs_s, rhs_s, bt_s, c_s, acs_s, sa_s, eel_s, dtr_s, acsr_s, hb_s, hrun_s,
                  ka_s, kb_s, va_s, vb_s, ext_s, rseg_s, esel_s, sc_s, ex_s, *, seqlen):
    n_chunks = seqlen // CHUNK
    hpg = HEADS_PER_GROUP
    lane = lax.broadcasted_iota(jnp.int32, (CHUNK, LANES), 1)
    row = lax.broadcasted_iota(jnp.int32, (CHUNK, LANES), 0)
    lane_lo = lane < HEAD_DIM
    tril_incl = lane <= row
    diag = lane == row
    tri_bf16 = jnp.where(tril_incl, 1.0, 0.0).astype(BF16)
    is_fwd = lane < hpg
    row16 = lax.broadcasted_iota(jnp.int32, (DT_G, CHUNK), 0)

    srow = lax.broadcasted_iota(jnp.int32, (CHUNK, SEG_COLS), 0)
    sblk = lax.broadcasted_iota(jnp.int32, (CHUNK, SEG_COLS), 1) // CHUNK
    sidx = (sblk % 2) * hpg + sblk // 2
    rseg_s[...] = jnp.where((srow < SEG_ONES) & (srow % SPLIT == sidx), 1.0, 0.0).astype(BF16)
    erow = lax.broadcasted_iota(jnp.int32, (CHUNK, 2 * XS_G), 0)
    ecol = lax.broadcasted_iota(jnp.int32, (CHUNK, 2 * XS_G), 1)
    eidx = (ecol // XS_G) * hpg + (ecol % XS_G) // SSM_HEAD_DIM
    esel_s[...] = jnp.where((erow < 2 * SPLIT) & (erow % SPLIT == eidx), 1.0, 0.0).astype(BF16)
    bidx = (lax.broadcasted_iota(jnp.int32, (DT_G, SEG_COLS), 1) // CHUNK)
    band_sel = lax.broadcasted_iota(jnp.int32, (DT_G, SEG_COLS), 0) == (bidx % 2) * hpg + bidx // 2

    zpad = jnp.zeros((WINDOW, LANES), BF16)
    for s in (ka_s, kb_s, va_s, vb_s):
        s[0:WINDOW, :] = zpad
        s[seqlen + WINDOW:seqlen + 2 * WINDOW, :] = zpad

    def fill_kv(c, carry):
        r0 = pl.multiple_of(c * CHUNK, CHUNK)
        kv = kv_ref[pl.ds(r0, CHUNK), :]
        swapped = jnp.concatenate([kv[:, HEAD_DIM:], kv[:, :HEAD_DIM]], axis=1)
        zero = jnp.zeros_like(kv)
        dst = pl.ds(r0 + WINDOW, CHUNK)
        ka_s[dst, :] = jnp.where(lane_lo, kv, zero)
        kb_s[dst, :] = jnp.where(lane_lo, zero, swapped)
        va_s[dst, :] = jnp.where(lane_lo, swapped, zero)
        vb_s[dst, :] = jnp.where(lane_lo, zero, kv)
        return carry

    lax.fori_loop(0, n_chunks, fill_kv, 0)

    kidx = lax.broadcasted_iota(jnp.int32, (WINDOW, 3 * WINDOW), 1)
    qidx = lax.broadcasted_iota(jnp.int32, (WINDOW, 3 * WINDOW), 0)
    rel = kidx - qidx
    band = (rel >= 0) & (rel <= 2 * WINDOW)

    def attn_pair(n, t):
        r0 = pl.multiple_of(n * WINDOW, WINDOW)
        kpos = kidx + (n - 1) * WINDOW
        mask = band & (kpos >= 0) & (kpos < seqlen)
        win = pl.ds(r0, 3 * WINDOW)
        qp = q_ref[pl.ds(r0, WINDOW), t * LANES:(t + 1) * LANES]
        halves = ((ka_s, va_s), (kb_s, vb_s))
        for half, (k_s, _) in enumerate(halves):
            sc_s[half] = jnp.where(mask, _dot_nt(qp, k_s[win, :]), -jnp.inf)
        ms, dens = [], []
        for half in range(2):
            sk = sink_ref[2 * t + half:2 * t + half + 1, 0:1] * LOG2E
            ms.append(jnp.maximum(jnp.max(sc_s[half], axis=-1, keepdims=True), sk))
        for half in range(2):
            sk = sink_ref[2 * t + half:2 * t + half + 1, 0:1] * LOG2E
            e = jnp.exp2(sc_s[half] - ms[half])
            dens.append(jnp.sum(e, axis=-1, keepdims=True) + jnp.exp2(sk - ms[half]))
            ex_s[half] = e.astype(BF16)
        out = None
        for half, (_, v_s) in enumerate(halves):
            o = _dot(ex_s[half], v_s[win, :]) * (1.0 / dens[half])
            out = o if out is None else out + o
        attn_ref[pl.ds(r0, WINDOW), t * LANES:(t + 1) * LANES] = out.astype(BF16)

    a_row = -jnp.exp(alog_ref[...])

    def decay_rows(dt_r, acs_r):
        edge = jnp.where(row16[:, 0:1] < hpg, acs_r[:, CHUNK - 1:CHUNK], acs_r[:, 0:1])
        return dt_r * jnp.exp(edge - acs_r)

    def edge_decay(acs, first_head, anchor_row):
        dec = jnp.exp(acs[anchor_row:anchor_row + 1, :])
        return [_pair_expand(dec, first_head + 2 * j, lane_lo[0:1, :]) for j in range(PAIRS)]

    def weighted_bt(bt, wst_r, first_head, j):
        h0 = first_head + 2 * j
        return jnp.concatenate([(bt * wst_r[h0:h0 + 1, :]).astype(BF16),
                                (bt * wst_r[h0 + 1:h0 + 2, :]).astype(BF16)], axis=1)

    hrun_s[...] = jnp.zeros_like(hrun_s)

    def loop1(i, carry):
        c = n_chunks - 1 - i
        r0 = pl.multiple_of(c * CHUNK, CHUNK)
        rows = pl.ds(r0, CHUNK)
        p0 = pl.multiple_of(jnp.maximum(r0 - BF16_ROWS, 0), BF16_ROWS)
        n0 = pl.multiple_of(jnp.minimum(r0 + CHUNK, seqlen - BF16_ROWS), BF16_ROWS)
        prev = xbc_ref[pl.ds(p0, BF16_ROWS), :].astype(F32)
        nxt = xbc_ref[pl.ds(n0, BF16_ROWS), :].astype(F32)
        ext_s[0:BF16_ROWS, :] = jnp.where(c > 0, prev, 0.0)
        ext_s[BF16_ROWS:BF16_ROWS + CHUNK, :] = xbc_ref[rows, :].astype(F32)
        ext_s[BF16_ROWS + CHUNK:, :] = jnp.where(c < n_chunks - 1, nxt, 0.0)
        acc = jnp.broadcast_to(cb_ref[...], (CHUNK, XBC_G))
        for j in range(D_CONV):
            lo = BF16_ROWS - CONV_PAD + j
            acc = acc + ext_s[lo:lo + CHUNK, :] * cw_ref[j:j + 1, :]
        act = acc * _sigmoid(acc)
        xs = act[:, :XS_G]
        xs_s[rows, :] = xs
        bt = act[:, XS_G:XS_G + D_STATE].T
        bt_s[c] = bt.astype(BF16)
        c_s[rows, :] = act[:, XS_G + D_STATE:].astype(BF16)
        xsb = xs.astype(BF16)
        zero = jnp.zeros((CHUNK, LANES), BF16)
        rhs = []
        for j in range(PAIRS):
            tile = xsb[:, j * LANES:(j + 1) * LANES]
            rhs.append(jnp.concatenate([jnp.where(lane_lo, tile, zero),
                                        jnp.where(lane_lo, zero, tile)], axis=0))
            rhs_s[c, j] = rhs[j]

        raw = dt_ref[rows, :] + dtb_ref[...]
        dt = jnp.maximum(raw, 0.0) + jnp.log1p(jnp.exp(-jnp.abs(raw)))
        da = dt * a_row
        d1, d2, d3 = _split3(da)
        incl = _dot(tri_bf16, d1) + _dot(tri_bf16, d2) + _dot(tri_bf16, d3)
        total = incl[CHUNK - 1:CHUNK, :]
        acs = jnp.where(is_fwd, incl, total - incl + da)
        acs_s[rows, :] = acs
        p1, p2, p3 = (p.astype(F32) for p in _split3(acs * LOG2E))
        sa = jnp.where(lane < SPLIT, p1,
                       jnp.where(lane < 2 * SPLIT, pltpu.roll(p2, SPLIT, 1),
                                 jnp.where(lane < SEG_ONES, pltpu.roll(p3, 2 * SPLIT, 1),
                                           jnp.where(lane < 2 * SEG_ONES, 1.0, 0.0))))
        sa_s[rows, :] = sa.astype(BF16)
        eacs = jnp.exp(acs)
        hi = eacs.astype(BF16).astype(F32)
        eel_s[rows, :] = jnp.where(lane < SPLIT, hi, pltpu.roll(eacs - hi, SPLIT, 1)).astype(BF16)
        dt_r = dt.T[0:DT_G, :]
        acs_r = acs.T[0:DT_G, :]
        dtr_s[c] = dt_r
        acsr_s[c] = acs_r

        hb_s[c] = hrun_s[...].astype(BF16)
        wst_r = decay_rows(dt_r, acs_r)
        dec = edge_decay(acs, hpg, 0)
        for j in range(PAIRS):
            cols = slice(j * LANES, (j + 1) * LANES)
            st = _dot(weighted_bt(bt, wst_r, hpg, j), rhs[j])
            hrun_s[:, cols] = hrun_s[:, cols] * dec[j] + st

        attn_pair(c, 0)
        return carry

    lax.fori_loop(0, n_chunks, loop1, 0)

    hrun_s[...] = jnp.zeros_like(hrun_s)

    def loop2(c, carry):
        rows = pl.ds(pl.multiple_of(c * CHUNK, CHUNK), CHUNK)
        cmat = c_s[rows, :]
        bt16 = bt_s[c]
        cb = _dot(cmat, bt16)
        acs_r = acsr_s[c]
        dt_r = dtr_s[c]
        hcat = jnp.concatenate([hrun_s[...].astype(BF16), hb_s[c]], axis=1)
        yoff = _dot(cmat, hcat)
        ee = _dot(eel_s[rows, :], esel_s[...])

        for i, piece in enumerate(_split3(acs_r * LOG2E)):
            neg = -piece.astype(F32)
            bandrows = jnp.where(band_sel, jnp.concatenate([neg] * (2 * hpg), axis=1), 0.0)
            rseg_s[SEG_ONES + i * SPLIT:SEG_ONES + (i + 1) * SPLIT, :] = bandrows.astype(BF16)

        sa = sa_s[rows, :]
        wst_r = decay_rows(dt_r, acs_r)
        dec = edge_decay(acs_s[rows, :], 0, CHUNK - 1)
        bt = bt16.astype(F32)
        pieces = []
        ysq = jnp.zeros((CHUNK, LANES), F32)
        for j in range(PAIRS):
            cols = slice(j * LANES, (j + 1) * LANES)
            ws = []
            for h in (2 * j, 2 * j + 1):
                seg2 = _dot(sa, rseg_s[:, 2 * h * CHUNK:(2 * h + 2) * CHUNK])
                seg = jnp.where(tril_incl, seg2[:, :CHUNK], seg2[:, CHUNK:])
                dtf = dt_r[h:h + 1, :]
                dtb = dt_r[hpg + h:hpg + h + 1, :]
                dsel = jnp.where(diag, dtf + dtb, jnp.where(tril_incl, dtf, dtb))
                ws.append((cb * jnp.exp2(seg) * dsel).astype(BF16))
            lhs = jnp.concatenate([jnp.concatenate(ws, axis=1), weighted_bt(bt, wst_r, 0, j)], axis=0)
            res = _dot(lhs, rhs_s[c, j])
            y = res[:CHUNK]
            y = y + yoff[:, cols] * ee[:, cols]
            y = y + yoff[:, XS_G + j * LANES:XS_G + (j + 1) * LANES] * ee[:, XS_G + j * LANES:
                                                                          XS_G + (j + 1) * LANES]
            y = y + xs_s[rows, cols] * dskip_ref[:, cols]
            z = z_ref[rows, cols].astype(F32)
            y = y * (z * _sigmoid(z))
            ysq = ysq + y * y
            pieces.append(y)
            hrun_s[:, cols] = hrun_s[:, cols] * dec[j] + res[CHUNK:]
        scale = lax.rsqrt(jnp.sum(ysq, axis=-1, keepdims=True) * (1.0 / XS_G) + EPS)
        for j in range(PAIRS):
            cols = slice(j * LANES, (j + 1) * LANES)
            y_ref[rows, cols] = (pieces[j] * scale * snorm_ref[:, cols]).astype(BF16)

        attn_pair(c, 1)
        return carry

    lax.fori_loop(0, n_chunks, loop2, 0)


def _mixer(q, kv, xbc, z, dt, cw, cb, dtb, alog, dskip, snorm, sink, batch, seqlen):
    n_chunks = seqlen // CHUNK
    seq_col = lambda b, g: (b, 0, g)
    grp = lambda b, g: (g, 0, 0)
    sq = pl.Squeezed()
    return pl.pallas_call(
        functools.partial(_mixer_kernel, seqlen=seqlen),
        grid=(batch, N_GROUPS),
        in_specs=[
            pl.BlockSpec((sq, seqlen, Q_G), seq_col),
            pl.BlockSpec((sq, seqlen, KV_G), seq_col),
            pl.BlockSpec((sq, seqlen, XBC_G), seq_col),
            pl.BlockSpec((sq, seqlen, XS_G), seq_col),
            pl.BlockSpec((sq, seqlen, LANES), seq_col),
            pl.BlockSpec((sq, D_CONV, XBC_G), grp),
            pl.BlockSpec((sq, 1, XBC_G), grp),
            pl.BlockSpec((sq, 1, LANES), grp),
            pl.BlockSpec((sq, 1, LANES), grp),
            pl.BlockSpec((sq, 1, XS_G), grp),
            pl.BlockSpec((sq, 1, XS_G), grp),
            pl.BlockSpec((sq, Q_PER_KV, LANES), grp),
        ],
        out_specs=[
            pl.BlockSpec((sq, seqlen, Q_G), seq_col),
            pl.BlockSpec((sq, seqlen, XS_G), seq_col),
        ],
        out_shape=[
            jax.ShapeDtypeStruct((batch, seqlen, D_ATTN), BF16),
            jax.ShapeDtypeStruct((batch, seqlen, D_SSM), BF16),
        ],
        scratch_shapes=[
            pltpu.VMEM((seqlen, XS_G), F32),
            pltpu.VMEM((n_chunks, PAIRS, 2 * CHUNK, LANES), BF16),
            pltpu.VMEM((n_chunks, D_STATE, CHUNK), BF16),
            pltpu.VMEM((seqlen, D_STATE), BF16),
            pltpu.VMEM((seqlen, LANES), F32),
            pltpu.VMEM((seqlen, LANES), BF16),
            pltpu.VMEM((seqlen, LANES), BF16),
            pltpu.VMEM((n_chunks, DT_G, CHUNK), F32),
            pltpu.VMEM((n_chunks, DT_G, CHUNK), F32),
            pltpu.VMEM((n_chunks, D_STATE, XS_G), BF16),
            pltpu.VMEM((D_STATE, XS_G), F32),
            pltpu.VMEM((seqlen + 2 * WINDOW, LANES), BF16),
            pltpu.VMEM((seqlen + 2 * WINDOW, LANES), BF16),
            pltpu.VMEM((seqlen + 2 * WINDOW, LANES), BF16),
            pltpu.VMEM((seqlen + 2 * WINDOW, LANES), BF16),
            pltpu.VMEM((CHUNK + 2 * BF16_ROWS, XBC_G), F32),
            pltpu.VMEM((CHUNK, SEG_COLS), BF16),
            pltpu.VMEM((CHUNK, 2 * XS_G), BF16),
            pltpu.VMEM((2, WINDOW, 3 * WINDOW), F32),
            pltpu.VMEM((2, WINDOW, 3 * WINDOW), BF16),
        ],
        compiler_params=pltpu.CompilerParams(
            dimension_semantics=("arbitrary", "arbitrary"), vmem_limit_bytes=VMEM_LIMIT_BYTES),
        name="mixer",
    )(q, kv, xbc, z, dt, cw, cb, dtb, alog, dskip, snorm, sink)


def _out_ffn_kernel(x_ref, attn_ref, y_ref, ga_ref, wo_ref, gm_ref, wu_ref, wd_ref, o_ref):
    a = attn_ref[...].astype(F32)
    an = (a * lax.rsqrt(jnp.mean(a * a, axis=-1, keepdims=True) + EPS) * ga_ref[...]).astype(BF16)
    x1 = x_ref[...] + _dot(an, wo_ref[0:D_ATTN, :]) + _dot(y_ref[...], wo_ref[D_ATTN:, :])
    h = (x1 * lax.rsqrt(jnp.mean(x1 * x1, axis=-1, keepdims=True) + EPS) * gm_ref[...]).astype(BF16)
    o_ref[...] = x1
    step = D_FF // 4
    for c0 in range(0, D_FF, step):
        u = jnp.maximum(_dot(h, wu_ref[:, c0:c0 + step]), 0.0)
        o_ref[...] += _dot((u * u).astype(BF16), wd_ref[c0:c0 + step, :])


def _out_ffn(x, attn, y, ga, wo, gm, wu, wd, seqlen):
    tokens = x.shape[0]
    tm = min(TOKEN_TILE, seqlen)
    row = lambda i: (i, 0)
    const = lambda i: (0, 0)
    return pl.pallas_call(
        _out_ffn_kernel,
        grid=(tokens // tm,),
        in_specs=[
            pl.BlockSpec((tm, D_MODEL), row),
            pl.BlockSpec((tm, D_ATTN), row),
            pl.BlockSpec((tm, D_SSM), row),
            pl.BlockSpec((1, D_ATTN), const),
            _resident_spec((D_ATTN + D_SSM, D_MODEL)),
            pl.BlockSpec((1, D_MODEL), const),
            _resident_spec((D_MODEL, D_FF)),
            _resident_spec((D_FF, D_MODEL)),
        ],
        out_specs=pl.BlockSpec((tm, D_MODEL), row),
        out_shape=jax.ShapeDtypeStruct((tokens, D_MODEL), F32),
        compiler_params=pltpu.CompilerParams(
            dimension_semantics=("arbitrary",), vmem_limit_bytes=VMEM_LIMIT_BYTES),
        name="out_ffn",
    )(x, attn, y, ga, wo, gm, wu, wd)


def _rope_tables(seqlen):
    pos = jnp.arange(seqlen, dtype=F32)
    inv_freq = ROPE_THETA ** (-jnp.arange(0, ROT_DIM, 2, dtype=F32) / ROT_DIM)
    ang = pos[:, None] * inv_freq[None, :]
    cos, sin = jnp.cos(ang), jnp.sin(ang)
    ones = jnp.ones((seqlen, HEAD_DIM - ROT_DIM), F32)
    zeros = jnp.zeros((seqlen, HEAD_DIM - ROT_DIM), F32)
    zhalf = jnp.zeros((seqlen, ROT_HALF), F32)
    head = jnp.stack([
        jnp.concatenate([cos, cos, ones], axis=1),
        jnp.concatenate([zhalf, sin, zeros], axis=1),
        jnp.concatenate([-sin, zhalf, zeros], axis=1),
    ])
    ident = jnp.stack([jnp.ones((seqlen, HEAD_DIM), F32),
                       jnp.zeros((seqlen, HEAD_DIM), F32),
                       jnp.zeros((seqlen, HEAD_DIM), F32)])
    rope_q = jnp.concatenate([head, head], axis=2) * (ATTN_SCALE * LOG2E)
    rope_kv = jnp.concatenate([head, ident], axis=2)
    return rope_q, rope_kv


def _head_sum_matrix():
    blk = np.arange(2 * LANES) // HEAD_DIM
    return jnp.asarray(blk[:, None] == blk[None, :], dtype=BF16)


def _regroup(w_in, conv_w, conv_b, dt_bias, a_log, d_skip, ssm_norm, q_norm, k_norm, sink):
    depth = w_in.shape[0]
    o1 = D_ATTN
    o2 = o1 + D_KV
    o3 = o2 + D_KV
    o4 = o3 + CONV_DIM
    o5 = o4 + D_SSM

    def group_xbc(t):
        lead = t.shape[:-1]
        xs = t[..., :D_SSM].reshape(*lead, N_GROUPS, XS_G)
        b = t[..., D_SSM:D_SSM + N_GROUPS * D_STATE].reshape(*lead, N_GROUPS, D_STATE)
        c = t[..., D_SSM + N_GROUPS * D_STATE:].reshape(*lead, N_GROUPS, D_STATE)
        return jnp.concatenate([xs, b, c], axis=-1)

    def group_dt(t):
        lead = t.shape[:-1]
        t = t.reshape(*lead, 2, N_GROUPS, HEADS_PER_GROUP)
        t = jnp.moveaxis(t, -3, -2).reshape(*lead, N_GROUPS, DT_G)
        pad = [(0, 0)] * (t.ndim - 1) + [(0, LANES - DT_G)]
        return jnp.pad(t, pad)

    wq = w_in[..., :o1]
    wk = w_in[..., o1:o2].reshape(depth, D_MODEL, N_KV_HEADS, HEAD_DIM)
    wv = w_in[..., o2:o3].reshape(depth, D_MODEL, N_KV_HEADS, HEAD_DIM)
    wkv = jnp.concatenate([wk, wv], axis=-1).reshape(depth, D_MODEL, 2 * D_KV)
    wxbc = group_xbc(w_in[..., o3:o4]).reshape(depth, D_MODEL, CONV_DIM)
    wz = w_in[..., o4:o5]
    wdt = group_dt(w_in[..., o5:]).reshape(depth, D_MODEL, N_GROUPS * LANES)
    w_all = jnp.concatenate([wq, wkv, wxbc, wz, wdt], axis=-1).astype(BF16)

    cw = jnp.moveaxis(group_xbc(conv_w), -2, 1)
    cb = group_xbc(conv_b)[:, :, None, :]
    dtb = group_dt(dt_bias.reshape(depth, 2 * N_SSM_HEADS))[:, :, None, :]
    alog = group_dt(a_log.reshape(depth, 2 * N_SSM_HEADS))[:, :, None, :]
    dskip = jnp.repeat(d_skip, SSM_HEAD_DIM, axis=-1).reshape(depth, N_GROUPS, 1, XS_G)
    snorm = ssm_norm.reshape(depth, N_GROUPS, 1, XS_G)
    gq = jnp.tile(q_norm, (1, N_HEADS))[:, None, :]
    gkv = jnp.tile(jnp.concatenate([k_norm, jnp.ones_like(k_norm)], axis=-1),
                   (1, N_KV_HEADS))[:, None, :]
    sink_b = jnp.broadcast_to(sink.reshape(depth, N_KV_HEADS, Q_PER_KV, 1),
                              (depth, N_KV_HEADS, Q_PER_KV, LANES))
    return w_all, cw, cb, dtb, alog, dskip, snorm, gq, gkv, sink_b


def _trunk(x, params, tables):
    batch, seqlen, _ = x.shape
    (norm_mix, w_all, cw, cb, dtb, alog, dskip, snorm, gq, gkv, sink_b, attn_norm, w_out,
     norm_mlp, w_up, w_down) = params
    rope_q, rope_kv, ones = tables
    tokens = batch * seqlen
    x = x.reshape(tokens, D_MODEL)
    for i in range(w_all.shape[0]):
        q, kv, xbc, z, dt = _in_proj(x, norm_mix[i][None, :], w_all[i], gq[i], gkv[i],
                                     rope_q, rope_kv, ones, seqlen)
        shape3 = lambda t: t.reshape(batch, seqlen, t.shape[-1])
        attn, y = _mixer(shape3(q), shape3(kv), shape3(xbc), shape3(z), shape3(dt),
                         cw[i], cb[i], dtb[i], alog[i], dskip[i], snorm[i], sink_b[i],
                         batch, seqlen)
        x = _out_ffn(x, attn.reshape(tokens, D_ATTN), y.reshape(tokens, D_SSM),
                     attn_norm[i][None, :], w_out[i], norm_mlp[i][None, :], w_up[i], w_down[i],
                     seqlen)
    return x.reshape(batch, seqlen, D_MODEL)


def kernel(x_prompt, x_sample, norm_mix, w_in, conv_w, conv_b, dt_bias, a_log, d_skip, ssm_norm,
           q_norm, k_norm, sink, attn_norm, w_out, norm_mlp, w_up, w_down):
    (w_all, cw, cb, dtb, alog, dskip, snorm, gq, gkv, sink_b) = _regroup(
        w_in, conv_w, conv_b, dt_bias, a_log, d_skip, ssm_norm, q_norm, k_norm, sink)
    params = (norm_mix, w_all, cw, cb, dtb, alog, dskip, snorm, gq, gkv, sink_b, attn_norm,
              w_out.astype(BF16), norm_mlp, w_up.astype(BF16), w_down.astype(BF16))
    ones = _head_sum_matrix()
    outs = []
    for x in (x_prompt, x_sample):
        rope_q, rope_kv = _rope_tables(x.shape[1])
        outs.append(_trunk(x, params, (rope_q, rope_kv, ones)))
    return tuple(outs)
```
